```python
import math
import functools
import jax
import jax.numpy as jnp
from jax import lax
import numpy as np

D_MODEL = 2048
BATCH = 2
SEQ = 4096
DEPTH = 2
DEC_BATCH = 128
DEC_SEQ = 4
PAST_LEN = 2048
PAGE_SIZE = 128

N_EVEN = (DEPTH + 1) // 2
N_ODD = DEPTH // 2
M_HEADS = 32
M_HEAD_DIM = 64
M_INNER = M_HEADS * M_HEAD_DIM
M_GROUPS = 4
M_STATE = 128
M_CONV = 4
M_CONV_DIM = M_INNER + 2 * M_GROUPS * M_STATE
SSD_CHUNK = 128
SB_HEADS = 16
SB_HEAD_DIM = 128
SB_INNER = SB_HEADS * SB_HEAD_DIM
DF_HEADS = 16
DF_HEAD_DIM = 64
DF_V_DIM = 2 * DF_HEAD_DIM
DF_INNER = DF_HEADS * DF_V_DIM
Q_BLOCK = 128
EPS = 1e-6
E_SIZES = (M_INNER, M_CONV_DIM, M_HEADS, SB_INNER, SB_INNER, SB_INNER, SB_INNER)
E_IN = sum(E_SIZES)
O_SIZES = (DF_INNER, 2 * DF_HEADS * DF_HEAD_DIM, 2 * DF_HEADS * DF_HEAD_DIM, DF_INNER)
O_IN = sum(O_SIZES)

kernel_name = "hybrid_ssd_stickbreak_diffattn_step"

f32 = jnp.float32


def rms_norm(x, w):
    xf = x.astype(f32)
    y = xf * lax.rsqrt(jnp.mean(xf * xf, axis=-1, keepdims=True) + EPS)
    return (y * w.astype(f32)).astype(x.dtype)


def segsum(a):
    t = a.shape[-1]
    ar = jnp.broadcast_to(a[..., None], a.shape + (t,))
    ar = jnp.where(jnp.tril(jnp.ones((t, t), bool), -1), ar, 0.0)
    cs = jnp.cumsum(ar, axis=-2)
    return jnp.where(jnp.tril(jnp.ones((t, t), bool)), cs, -jnp.inf)


def ssd_scan(x, dt, a_head, bm, cm, h0):
    b, l, h, p = x.shape
    g, n = bm.shape[2], bm.shape[3]
    t = SSD_CHUNK if l % SSD_CHUNK == 0 else l
    c = l // t
    hg = h // g
    xs = (x * dt[..., None]).reshape(b, c, t, g, hg, p)
    a = jnp.moveaxis((dt * a_head).reshape(b, c, t, g, hg), 2, -1)
    a_cs = jnp.cumsum(a, axis=-1)
    bc = bm.reshape(b, c, t, g, n)
    cc = cm.reshape(b, c, t, g, n)
    lmat = jnp.exp(segsum(a))
    cb = jnp.einsum('bctgn,bcsgn->bcgts', cc, bc)
    y_diag = jnp.einsum('bcgts,bcgkts,bcsgkp->bctgkp', cb, lmat, xs)
    decay_states = jnp.exp(a_cs[..., -1:] - a_cs)
    states = jnp.einsum('bcsgn,bcgks,bcsgkp->bcgkpn', bc, decay_states, xs)
    states = jnp.concatenate([h0.reshape(b, 1, g, hg, p, n), states], axis=1)
    chunk_a = jnp.pad(a_cs[..., -1], [(0, 0), (1, 0), (0, 0), (0, 0)])
    decay_chunk = jnp.exp(segsum(jnp.moveaxis(chunk_a, 1, -1)))
    new_states = jnp.einsum('bgkzc,bcgkpn->bzgkpn', decay_chunk, states)
    states_in = new_states[:, :-1]
    h_final = new_states[:, -1]
    y_off = jnp.einsum('bctgn,bcgkpn,bcgkt->bctgkp', cc, states_in, jnp.exp(a_cs))
    y = (y_diag + y_off).reshape(b, l, h, p)
    return y, h_final.reshape(b, h, p, n)


def causal_conv(u, buf, w, bias):
    up = jnp.concatenate([buf.astype(u.dtype), u], axis=1)
    y = lax.conv_general_dilated(up, w[:, None, :].astype(u.dtype), window_strides=(1,), padding='VALID',
                                 dimension_numbers=('NWC', 'WIO', 'NWC'), feature_group_count=u.shape[-1])
    return jax.nn.silu(y + bias), up[:, -(M_CONV - 1):]


def gather_pages(pool, page_table):
    g = pool[page_table]
    return g.reshape((g.shape[0], g.shape[1] * g.shape[2]) + g.shape[3:])


def stick_breaking(q, k, v, q_pos, k_pos):
    z = jnp.einsum('bqhd,bkhd->bhqk', q, k).astype(f32) * (SB_HEAD_DIM ** -0.5)
    mask = k_pos[None, :] < q_pos[:, None]
    u = jnp.where(mask, jax.nn.log_sigmoid(-z), 0.0)
    after = lax.cumsum(u, axis=3, reverse=True) - u
    w = jnp.where(mask, jnp.exp(jax.nn.log_sigmoid(z) + after), 0.0)
    return jnp.einsum('bhqk,bkhd->bqhd', w.astype(v.dtype), v)


def diff_attention(q, k, v, q_pos, k_pos, lam):
    s = jnp.einsum('bqhcd,bkhcd->bhcqk', q, k).astype(f32) * (DF_HEAD_DIM ** -0.5)
    mask = k_pos[None, :] <= q_pos[:, None]
    p = jax.nn.softmax(jnp.where(mask, s, -jnp.inf), axis=-1)
    a = p[:, :, 0] - lam * p[:, :, 1]
    return jnp.einsum('bhqk,bkhe->bqhe', a.astype(v.dtype), v)


def blocked_attend(fn, q, k_all, v_all, past_len):
    l = q.shape[1]
    qb = Q_BLOCK if l % Q_BLOCK == 0 else l
    outs = []
    for i in range(l // qb):
        t0 = i * qb
        kend = past_len + t0 + qb
        q_pos = past_len + t0 + jnp.arange(qb)
        k_pos = jnp.arange(kend)
        outs.append(fn(q[:, t0:t0 + qb], k_all[:, :kend], v_all[:, :kend], q_pos, k_pos))
    return jnp.concatenate(outs, axis=1)


def even_layer(x, h0, conv0, k_past, v_past, norm_w, w_in, conv_w, conv_b, dt_bias, a_log, d_skip, m_norm, w_out):
    b, l, _ = x.shape
    proj = rms_norm(x, norm_w) @ w_in
    z_m, xbc, dt_raw, z_sb, q, k, v = jnp.split(proj, np.cumsum(E_SIZES)[:-1].tolist(), axis=-1)
    xbc, conv_new = causal_conv(xbc, conv0, conv_w, conv_b)
    xm = xbc[..., :M_INNER].reshape(b, l, M_HEADS, M_HEAD_DIM)
    bm = xbc[..., M_INNER:M_INNER + M_GROUPS * M_STATE].reshape(b, l, M_GROUPS, M_STATE)
    cm = xbc[..., M_INNER + M_GROUPS * M_STATE:].reshape(b, l, M_GROUPS, M_STATE)
    dt = jax.nn.softplus(dt_raw.astype(f32) + dt_bias.astype(f32))
    a_head = -jnp.exp(a_log.astype(f32))
    y, h_new = ssd_scan(xm.astype(f32), dt, a_head, bm.astype(f32), cm.astype(f32), h0.astype(f32))
    y = y + d_skip.astype(f32)[:, None] * xm.astype(f32)
    gz = (y.reshape(b, l, M_INNER) * jax.nn.silu(z_m.astype(f32))).reshape(b, l, M_GROUPS, M_INNER // M_GROUPS)
    gz = gz * lax.rsqrt(jnp.mean(gz * gz, axis=-1, keepdims=True) + EPS)
    y_m = (gz.reshape(b, l, M_INNER) * m_norm.astype(f32)).astype(x.dtype)
    q = q.reshape(b, l, SB_HEADS, SB_HEAD_DIM)
    k = k.reshape(b, l, SB_HEADS, SB_HEAD_DIM)
    v = v.reshape(b, l, SB_HEADS, SB_HEAD_DIM)
    k_all = jnp.concatenate([k_past.astype(k.dtype), k], axis=1)
    v_all = jnp.concatenate([v_past.astype(v.dtype), v], axis=1)
    o = blocked_attend(stick_breaking, q, k_all, v_all, k_past.shape[1])
    y_sb = (o.reshape(b, l, SB_INNER) * jax.nn.silu(z_sb)).astype(x.dtype)
    out = jnp.concatenate([y_m, y_sb], axis=-1) @ w_out
    return x + out, h_new.astype(h0.dtype), conv_new, k, v


def odd_layer(x, k_past, v_past, lambda_init, norm_w, w_in, lq1, lk1, lq2, lk2, sub_norm, w_out):
    b, l, _ = x.shape
    proj = rms_norm(x, norm_w) @ w_in
    z, q, k, v = jnp.split(proj, np.cumsum(O_SIZES)[:-1].tolist(), axis=-1)
    q = q.reshape(b, l, DF_HEADS, 2, DF_HEAD_DIM)
    k = k.reshape(b, l, DF_HEADS, 2, DF_HEAD_DIM)
    v = v.reshape(b, l, DF_HEADS, DF_V_DIM)
    lam = (jnp.exp(jnp.sum(lq1.astype(f32) * lk1.astype(f32)))
           - jnp.exp(jnp.sum(lq2.astype(f32) * lk2.astype(f32))) + lambda_init)
    k_all = jnp.concatenate([k_past.astype(k.dtype), k], axis=1)
    v_all = jnp.concatenate([v_past.astype(v.dtype), v], axis=1)
    o = blocked_attend(functools.partial(diff_attention, lam=lam), q, k_all, v_all, k_past.shape[1])
    o = rms_norm(o, sub_norm) * (1.0 - lambda_init)
    out = (o.reshape(b, l, DF_INNER) * jax.nn.silu(z)).astype(x.dtype) @ w_out
    return x + out, k, v


def setup_inputs(seed: int = 0) -> dict:
    key = jax.random.key(seed)
    ks = jax.random.split(key, 32)

    def nrm(k, shape, scale):
        return jax.random.normal(k, shape, f32) * scale

    n_pages = PAST_LEN // PAGE_SIZE
    n_used = DEC_BATCH * n_pages
    n_pool = n_used + n_used // 4
    perm = jax.random.permutation(ks[8], n_pool)
    page_table = perm[:n_used].reshape(DEC_BATCH, n_pages).astype(jnp.int32)
    dt0 = jnp.exp(jax.random.uniform(ks[13], (N_EVEN, M_HEADS), f32, math.log(1e-3), math.log(1e-1)))
    return {
        'x_prompt': nrm(ks[0], (BATCH, SEQ, D_MODEL), 1.0),
        'x_sample': nrm(ks[1], (DEC_BATCH, DEC_SEQ, D_MODEL), 1.0),
        'state_ssm': nrm(ks[2], (N_EVEN, DEC_BATCH, M_HEADS, M_HEAD_DIM, M_STATE), M_STATE ** -0.5),
        'state_conv': nrm(ks[3], (N_EVEN, DEC_BATCH, M_CONV - 1, M_CONV_DIM), 1.0),
        'cache_sb_k': nrm(ks[4], (N_EVEN, n_pool, PAGE_SIZE, SB_HEADS, SB_HEAD_DIM), 1.0),
        'cache_sb_v': nrm(ks[5], (N_EVEN, n_pool, PAGE_SIZE, SB_HEADS, SB_HEAD_DIM), 1.0),
        'cache_df_k': nrm(ks[6], (N_ODD, n_pool, PAGE_SIZE, DF_HEADS, 2, DF_HEAD_DIM), 1.0),
        'cache_df_v': nrm(ks[7], (N_ODD, n_pool, PAGE_SIZE, DF_HEADS, DF_V_DIM), 1.0),
        'page_table': page_table,
        'norm_even': 1.0 + nrm(ks[9], (N_EVEN, D_MODEL), 0.02),
        'w_in_even': nrm(ks[10], (N_EVEN, D_MODEL, E_IN), D_MODEL ** -0.5),
        'conv_w': nrm(ks[11], (N_EVEN, M_CONV, M_CONV_DIM), M_CONV ** -0.5),
        'conv_b': nrm(ks[12], (N_EVEN, M_CONV_DIM), 0.01),
        'dt_bias': dt0 + jnp.log(-jnp.expm1(-dt0)),
        'a_log': jnp.log(jax.random.uniform(ks[14], (N_EVEN, M_HEADS), f32, 1.0, 16.0)),
        'd_skip': 1.0 + nrm(ks[15], (N_EVEN, M_HEADS), 0.1),
        'm_norm': 1.0 + nrm(ks[16], (N_EVEN, M_INNER), 0.02),
        'w_out_even': nrm(ks[17], (N_EVEN, M_INNER + SB_INNER, D_MODEL), (M_INNER + SB_INNER) ** -0.5),
        'norm_odd': 1.0 + nrm(ks[18], (N_ODD, D_MODEL), 0.02),
        'w_in_odd': nrm(ks[19], (N_ODD, D_MODEL, O_IN), D_MODEL ** -0.5),
        'lambda_q1': nrm(ks[20], (N_ODD, DF_HEAD_DIM), 0.1),
        'lambda_k1': nrm(ks[21], (N_ODD, DF_HEAD_DIM), 0.1),
        'lambda_q2': nrm(ks[22], (N_ODD, DF_HEAD_DIM), 0.1),
        'lambda_k2': nrm(ks[23], (N_ODD, DF_HEAD_DIM), 0.1),
        'df_norm': 1.0 + nrm(ks[24], (N_ODD, DF_V_DIM), 0.02),
        'w_out_odd': nrm(ks[25], (N_ODD, DF_INNER, D_MODEL), DF_INNER ** -0.5),
        'norm_f': 1.0 + nrm(ks[26], (D_MODEL,), 0.02),
    }


def reference(x_prompt, x_sample, state_ssm, state_conv, cache_sb_k, cache_sb_v, cache_df_k, cache_df_v, page_table,
              norm_even, w_in_even, conv_w, conv_b, dt_bias, a_log, d_skip, m_norm, w_out_even,
              norm_odd, w_in_odd, lambda_q1, lambda_k1, lambda_q2, lambda_k2, df_norm, w_out_odd, norm_f):
    xp, xs = x_prompt, x_sample
    bp = xp.shape[0]
    ssm_p, conv_p, sbk_p, sbv_p, dfk_p, dfv_p = [], [], [], [], [], []
    ssm_s, conv_s, sbk_s, sbv_s, dfk_s, dfv_s = [], [], [], [], [], []
    for i in range(DEPTH):
        j = i // 2
        if i % 2 == 0:
            ew = (norm_even[j], w_in_even[j], conv_w[j], conv_b[j], dt_bias[j], a_log[j], d_skip[j], m_norm[j],
                  w_out_even[j])
            kv0 = jnp.zeros((bp, 0, SB_HEADS, SB_HEAD_DIM), xp.dtype)
            h0 = jnp.zeros((bp,) + state_ssm.shape[2:], state_ssm.dtype)
            c0 = jnp.zeros((bp,) + state_conv.shape[2:], state_conv.dtype)
            xp, h, c, k, v = even_layer(xp, h0, c0, kv0, kv0, *ew)
            ssm_p.append(h); conv_p.append(c); sbk_p.append(k); sbv_p.append(v)
            k_past = gather_pages(cache_sb_k[j], page_table)
            v_past = gather_pages(cache_sb_v[j], page_table)
            xs, h, c, k, v = even_layer(xs, state_ssm[j], state_conv[j], k_past, v_past, *ew)
            ssm_s.append(h); conv_s.append(c); sbk_s.append(k); sbv_s.append(v)
        else:
            lambda_init = 0.8 - 0.6 * math.exp(-0.3 * i)
            ow = (norm_odd[j], w_in_odd[j], lambda_q1[j], lambda_k1[j], lambda_q2[j], lambda_k2[j], df_norm[j],
                  w_out_odd[j])
            k0 = jnp.zeros((bp, 0, DF_HEADS, 2, DF_HEAD_DIM), xp.dtype)
            v0 = jnp.zeros((bp, 0, DF_HEADS, DF_V_DIM), xp.dtype)
            xp, k, v = odd_layer(xp, k0, v0, lambda_init, *ow)
            dfk_p.append(k); dfv_p.append(v)
            k_past = gather_pages(cache_df_k[j], page_table)
            v_past = gather_pages(cache_df_v[j], page_table)
            xs, k, v = odd_layer(xs, k_past, v_past, lambda_init, *ow)
            dfk_s.append(k); dfv_s.append(v)
    y_prompt = rms_norm(xp, norm_f)
    y_sample = rms_norm(xs, norm_f)
    ssm_prompt = jnp.stack(ssm_p)
    conv_prompt = jnp.stack(conv_p)
    sbk_prompt = jnp.stack(sbk_p)
    sbv_prompt = jnp.stack(sbv_p)
    dfk_prompt = jnp.stack(dfk_p)
    dfv_prompt = jnp.stack(dfv_p)
    ssm_sample = jnp.stack(ssm_s)
    conv_sample = jnp.stack(conv_s)
    sbk_sample = jnp.stack(sbk_s)
    sbv_sample = jnp.stack(sbv_s)
    dfk_sample = jnp.stack(dfk_s)
    dfv_sample = jnp.stack(dfv_s)
    return (y_prompt, y_sample, ssm_prompt, conv_prompt, sbk_prompt, sbv_prompt, dfk_prompt, dfv_prompt,
            ssm_sample, conv_sample, sbk_sample, sbv_sample, dfk_sample, dfv_sample)
```

```python
import functools
import math

import jax
import jax.numpy as jnp
from jax import lax
from jax.experimental import pallas as pl
from jax.experimental.pallas import tpu as pltpu

f32 = jnp.float32
bf16 = jnp.bfloat16

D_MODEL = 2048
EPS = 1e-6
M_HEADS = 32
M_HEAD_DIM = 64
M_INNER = M_HEADS * M_HEAD_DIM
M_GROUPS = 4
M_STATE = 128
M_CONV = 4
M_CONV_DIM = M_INNER + 2 * M_GROUPS * M_STATE
SSD_CHUNK = 128
N_HEADS = 16
HEAD_W = 128
DF_HALF = 64
ATT_INNER = N_HEADS * HEAD_W
SB_SCALE = HEAD_W ** -0.5
DF_SCALE = DF_HALF ** -0.5
PAGE = 128
NEG_BIG = -1e30

LANES = 128
VMEM_LIMIT = 56 * 1024 * 1024


def _cparams(sem):
    return pltpu.CompilerParams(dimension_semantics=sem, vmem_limit_bytes=VMEM_LIMIT)


def _dot(a, b):
    return jnp.dot(a, b, preferred_element_type=f32)


def _dot_nt(a, b):
    return lax.dot_general(a, b, (((1,), (1,)), ((), ())), preferred_element_type=f32)


def _softplus(x):
    return jnp.maximum(x, 0.0) + jnp.log1p(jnp.exp(-jnp.abs(x)))


def _silu(x):
    return x * (1.0 / (1.0 + jnp.exp(-x)))


def _split2(x):
    hi = x.astype(bf16)
    lo = (x - hi.astype(f32)).astype(bf16)
    return hi, lo


def _split3(x):
    hi = x.astype(bf16)
    r = x - hi.astype(f32)
    mid = r.astype(bf16)
    lo = (r - mid.astype(f32)).astype(bf16)
    return hi, mid, lo


def _rmsnorm_kernel(x_ref, w_ref, o_ref):
    x = x_ref[...]
    ms = jnp.mean(x * x, axis=-1, keepdims=True)
    o_ref[...] = (x * lax.rsqrt(ms + EPS) * w_ref[...]).astype(o_ref.dtype)


def _rmsnorm(x, w, tm=512):
    m, d = x.shape
    tm = min(tm, m)
    return pl.pallas_call(
        _rmsnorm_kernel,
        grid=(m // tm,),
        in_specs=[pl.BlockSpec((tm, d), lambda i: (i, 0)), pl.BlockSpec((1, d), lambda i: (0, 0))],
        out_specs=pl.BlockSpec((tm, d), lambda i: (i, 0)),
        out_shape=jax.ShapeDtypeStruct((m, d), bf16),
        compiler_params=_cparams(("parallel",)),
        name="rmsnorm",
    )(x, w.reshape(1, d))


def _matmul_kernel(*refs, n_in, has_res, has_norm, out_kinds):
    xs = refs[:n_in]
    w_ref = refs[n_in]
    pos = n_in + 1
    res_ref = nw_ref = None
    if has_res:
        res_ref = refs[pos]
        pos += 1
    if has_norm:
        nw_ref = refs[pos]
        pos += 1
    outs = refs[pos:]
    acc = None
    k0 = 0
    for x_ref in xs:
        kk = x_ref.shape[1]
        part = _dot(x_ref[...], w_ref[k0:k0 + kk, :])
        acc = part if acc is None else acc + part
        k0 += kk
    if has_res:
        acc = acc + res_ref[...]
    if has_norm:
        ms = jnp.mean(acc * acc, axis=-1, keepdims=True)
        acc = acc * lax.rsqrt(ms + EPS) * nw_ref[...]
    for o_ref, kind in zip(outs, out_kinds):
        o_ref[...] = acc.astype(kind)


def _matmul(xs, w, res=None, norm_w=None, out_kinds=(f32,), tm=1024, tn=1024):
    m = xs[0].shape[0]
    k, n = w.shape
    tm = min(tm, m)
    tn = min(tn, n)
    if norm_w is not None:
        tn = n
    in_specs = [pl.BlockSpec((tm, x.shape[1]), lambda i, j: (i, 0)) for x in xs]
    in_specs.append(pl.BlockSpec((k, tn), lambda i, j: (0, j)))
    args = list(xs) + [w]
    if res is not None:
        in_specs.append(pl.BlockSpec((tm, tn), lambda i, j: (i, j)))
        args.append(res)
    if norm_w is not None:
        in_specs.append(pl.BlockSpec((1, tn), lambda i, j: (0, j)))
        args.append(norm_w.reshape(1, n))
    outs = pl.pallas_call(
        functools.partial(_matmul_kernel, n_in=len(xs), has_res=res is not None,
                          has_norm=norm_w is not None, out_kinds=out_kinds),
        grid=(m // tm, n // tn),
        in_specs=in_specs,
        out_specs=[pl.BlockSpec((tm, tn), lambda i, j: (i, j)) for _ in out_kinds],
        out_shape=[jax.ShapeDtypeStruct((m, n), kd) for kd in out_kinds],
        compiler_params=_cparams(("parallel", "arbitrary")),
        name="matmul",
    )(*args)
    return outs[0] if len(out_kinds) == 1 else tuple(outs)


def _ssd_kernel(xbc_ref, z_ref, dt_ref, conv0_ref, h0_ref, cw_ref, cb_ref, dtb_ref, alog_ref,
                dskip_ref, mnorm_ref, e64_ref, e128_ref,
                y_ref, hout_ref, ubuf, xc, state_t, ybuf, dtbuf, *, tv, nc):
    T = SSD_CHUNK
    c = pl.program_id(1)

    @pl.when(c == 0)
    def _init():
        ubuf[0:8, :] = conv0_ref[0]
        for j in range(M_HEADS // 2):
            blk = jnp.concatenate([h0_ref[0, 2 * j], h0_ref[0, 2 * j + 1]], axis=0)
            state_t[:, LANES * j:LANES * (j + 1)] = blk.T

    @pl.when(c > 0)
    def _shift():
        ubuf[0:8, :] = ubuf[T:T + 8, :]

    if tv < T:
        ubuf[8:8 + T, :] = jnp.zeros((T, M_CONV_DIM), f32)
        ubuf[8:8 + tv, :] = xbc_ref[0]
        dtbuf[...] = jnp.zeros((T, LANES), f32)
        dtbuf[0:tv, :] = dt_ref[0]
    else:
        ubuf[8:8 + T, :] = xbc_ref[0]
        dtbuf[...] = dt_ref[0]

    cw_chunk = 512
    for cc in range(M_CONV_DIM // cw_chunk):
        cs = slice(cc * cw_chunk, (cc + 1) * cw_chunk)
        acc = cb_ref[:, cs] + cw_ref[0:1, cs] * ubuf[5:5 + T, cs]
        acc = acc + cw_ref[1:2, cs] * ubuf[6:6 + T, cs]
        acc = acc + cw_ref[2:3, cs] * ubuf[7:7 + T, cs]
        acc = acc + cw_ref[3:4, cs] * ubuf[8:8 + T, cs]
        xc[:, cs] = _silu(acc)

    row = lax.broadcasted_iota(jnp.int32, (T, T), 0)
    col = lax.broadcasted_iota(jnp.int32, (T, T), 1)
    tril = col <= row
    tril_b = jnp.where(tril, 1.0, 0.0).astype(bf16)

    dt = _softplus(dtbuf[...] + dtb_ref[...])
    if tv < T:
        dt = jnp.where(row < tv, dt, 0.0)
    a = dt * (-jnp.exp(alog_ref[...]))
    a_hi, a_mid, a_lo = _split3(a)
    a_cs = _dot(tril_b, a_hi) + _dot(tril_b, a_mid) + _dot(tril_b, a_lo)
    a_cs_t = a_cs.T
    acs3 = jnp.concatenate(_split3(a_cs), axis=1)
    dt3 = jnp.concatenate(_split3(dt), axis=1)
    a_full = _dot(acs3, e64_ref[...])
    dt_full = _dot(dt3, e64_ref[...])
    alast_full = a_full[T - 1:T, :]
    ea_full = jnp.exp(a_full)
    ds_full = jnp.exp(alast_full - a_full)
    eal_full = jnp.exp(alast_full)

    lane = lax.broadcasted_iota(jnp.int32, (T, LANES), 1)
    lo_half = lane < M_HEAD_DIM

    for g in range(M_GROUPS):
        gs = slice(512 * g, 512 * (g + 1))
        bg = xc[:, M_INNER + M_STATE * g:M_INNER + M_STATE * (g + 1)]
        cg = xc[:, M_INNER + M_GROUPS * M_STATE + M_STATE * g:M_INNER + M_GROUPS * M_STATE + M_STATE * (g + 1)]
        bg_b = bg.astype(bf16)
        cg_b = cg.astype(bf16)
        cb = _dot_nt(cg_b, bg_b)
        a_wide = _dot(acs3, e128_ref[:, 1024 * g:1024 * (g + 1)])
        xs_g = xc[:, gs] * dt_full[:, gs]
        y_off = _dot(cg_b, state_t[:, gs].astype(bf16)) * ea_full[:, gs]
        ybuf[:, gs] = y_off + dskip_ref[:, gs] * xc[:, gs]
        for jj in range(4):
            j = 4 * g + jj
            xs_pair = xs_g[:, LANES * jj:LANES * (jj + 1)]
            yd = None
            for e in range(2):
                h = 2 * j + e
                hh = 2 * jj + e
                diff = a_wide[:, LANES * hh:LANES * (hh + 1)] - a_cs_t[h:h + 1, :]
                lmat = jnp.exp(jnp.where(tril, diff, -jnp.inf))
                mh = (cb * lmat).astype(bf16)
                keep = lo_half if e == 0 else jnp.logical_not(lo_half)
                xs_h = jnp.where(keep, xs_pair, 0.0).astype(bf16)
                part = _dot(mh, xs_h)
                yd = part if yd is None else yd + part
            ybuf[:, LANES * j:LANES * (j + 1)] += yd
        upd = _dot(bg.T.astype(bf16), (xs_g * ds_full[:, gs]).astype(bf16))
        state_t[:, gs] = state_t[:, gs] * eal_full[:, gs] + upd

    for g in range(M_GROUPS):
        gs = slice(512 * g, 512 * (g + 1))
        if tv < T:
            zg = z_ref[0][:, gs]
            gz = ybuf[0:tv, gs] * _silu(zg)
        else:
            gz = ybuf[:, gs] * _silu(z_ref[0][:, gs])
        ms = jnp.mean(gz * gz, axis=-1, keepdims=True)
        y_ref[0, :, gs] = (gz * lax.rsqrt(ms + EPS) * mnorm_ref[:, gs]).astype(y_ref.dtype)

    @pl.when(c == nc - 1)
    def _fin():
        for j in range(M_HEADS // 2):
            tr = state_t[:, LANES * j:LANES * (j + 1)].T
            hout_ref[0, 2 * j] = tr[0:M_HEAD_DIM]
            hout_ref[0, 2 * j + 1] = tr[M_HEAD_DIM:2 * M_HEAD_DIM]


def _ssd(xbc, z, dt, conv0, h0, conv_w, conv_b, dt_bias, a_log, d_skip, m_norm):
    b, l, _ = xbc.shape
    T = SSD_CHUNK
    tv = T if l % T == 0 else l
    nc = l // tv
    conv0p = jnp.concatenate([jnp.zeros((b, 5, M_CONV_DIM), f32), conv0], axis=1)
    cwp = jnp.concatenate([conv_w, jnp.zeros((8 - M_CONV, M_CONV_DIM), f32)], axis=0)
    pad = LANES - M_HEADS

    def padh(v):
        return jnp.concatenate([v, jnp.zeros((pad,), f32)]).reshape(1, LANES)

    hid = jnp.arange(LANES)[:, None]
    e64 = (jnp.arange(M_INNER)[None, :] // M_HEAD_DIM == hid).astype(bf16)
    e64 = jnp.concatenate([e64, e64, e64], axis=0)
    e128 = (jnp.arange(M_HEADS * LANES)[None, :] // LANES == hid).astype(bf16)
    e128 = jnp.concatenate([e128, e128, e128], axis=0)
    dskip_full = jnp.repeat(d_skip, M_HEAD_DIM).reshape(1, M_INNER)

    const = lambda shape: pl.BlockSpec(shape, lambda bi, ci: (0,) * len(shape))
    y, hout = pl.pallas_call(
        functools.partial(_ssd_kernel, tv=tv, nc=nc),
        grid=(b, nc),
        in_specs=[
            pl.BlockSpec((1, tv, M_CONV_DIM), lambda bi, ci: (bi, ci, 0)),
            pl.BlockSpec((1, tv, M_INNER), lambda bi, ci: (bi, ci, 0)),
            pl.BlockSpec((1, tv, LANES), lambda bi, ci: (bi, ci, 0)),
            pl.BlockSpec((1, 8, M_CONV_DIM), lambda bi, ci: (bi, 0, 0)),
            pl.BlockSpec((1, M_HEADS, M_HEAD_DIM, M_STATE), lambda bi, ci: (bi, 0, 0, 0)),
            const((8, M_CONV_DIM)), const((1, M_CONV_DIM)), const((1, LANES)), const((1, LANES)),
            const((1, M_INNER)), const((1, M_INNER)), const((3 * LANES, M_INNER)),
            const((3 * LANES, M_HEADS * LANES)),
        ],
        out_specs=[
            pl.BlockSpec((1, tv, M_INNER), lambda bi, ci: (bi, ci, 0)),
            pl.BlockSpec((1, M_HEADS, M_HEAD_DIM, M_STATE), lambda bi, ci: (bi, 0, 0, 0)),
        ],
        out_shape=[jax.ShapeDtypeStruct((b, l, M_INNER), bf16),
                   jax.ShapeDtypeStruct((b, M_HEADS, M_HEAD_DIM, M_STATE), f32)],
        scratch_shapes=[
            pltpu.VMEM((T + 8, M_CONV_DIM), f32),
            pltpu.VMEM((T, M_CONV_DIM), f32),
            pltpu.VMEM((M_STATE, M_INNER), f32),
            pltpu.VMEM((T, M_INNER), f32),
            pltpu.VMEM((T, LANES), f32),
        ],
        compiler_params=_cparams(("parallel", "arbitrary")),
        name="ssd",
    )(xbc, z, dt, conv0p, h0, cwp, conv_b.reshape(1, M_CONV_DIM), padh(dt_bias), padh(a_log),
      dskip_full, m_norm.reshape(1, M_INNER), e64, e128)
    return y, hout


def _sb_block(q, k, v, tri_after, carry, acc, mask):
    z = _dot_nt(q, k) * SB_SCALE
    sp = _softplus(z)
    u = -sp
    if mask is not None:
        u = jnp.where(mask, u, 0.0)
    u_hi, u_lo = _split2(u)
    after = _dot(u_hi, tri_after) + _dot(u_lo, tri_after)
    w = jnp.exp((z - sp) + after + carry)
    if mask is not None:
        w = jnp.where(mask, w, 0.0)
    acc = acc + _dot(w.astype(bf16), v)
    carry = carry + jnp.sum(u, axis=1, keepdims=True)
    return carry, acc


def _sb_prompt_kernel(q_ref, k_ref, v_ref, z_ref, o_ref, *, tq):
    i = pl.program_id(2)
    q = q_ref[...]
    row = lax.broadcasted_iota(jnp.int32, (tq, tq), 0)
    col = lax.broadcasted_iota(jnp.int32, (tq, tq), 1)
    tri_after = jnp.where(row > col, 1.0, 0.0).astype(bf16)
    diag_mask = col < row

    def kv(j):
        start = pl.multiple_of(j * tq, tq)
        return k_ref[pl.ds(start, tq), :], v_ref[pl.ds(start, tq), :]

    k, v = kv(i)
    carry, acc = _sb_block(q, k, v, tri_after, jnp.zeros((tq, 1), f32), jnp.zeros((tq, HEAD_W), f32), diag_mask)

    def body(jj, st):
        k, v = kv(i - 1 - jj)
        return _sb_block(q, k, v, tri_after, st[0], st[1], None)

    carry, acc = lax.fori_loop(0, i, body, (carry, acc))
    o_ref[...] = (acc * _silu(z_ref[...])).astype(o_ref.dtype)


def _sb_prompt(q, k, v, z, b, l, tq=256):
    tq = min(tq, l)
    nq = l // tq
    return pl.pallas_call(
        functools.partial(_sb_prompt_kernel, tq=tq),
        grid=(b, N_HEADS, nq),
        in_specs=[
            pl.BlockSpec((tq, HEAD_W), lambda bi, h, i: (bi * nq + i, h)),
            pl.BlockSpec((l, HEAD_W), lambda bi, h, i: (bi, h)),
            pl.BlockSpec((l, HEAD_W), lambda bi, h, i: (bi, h)),
            pl.BlockSpec((tq, HEAD_W), lambda bi, h, i: (bi * nq + i, h)),
        ],
        out_specs=pl.BlockSpec((tq, HEAD_W), lambda bi, h, i: (bi * nq + i, h)),
        out_shape=jax.ShapeDtypeStruct((b * l, ATT_INNER), bf16),
        compiler_params=_cparams(("parallel", "parallel", "arbitrary")),
        name="sb_prompt",
    )(q, k, v, z)


def _lambda_value(lamv_ref, lam_init):
    lv = lamv_ref[...]
    s1 = jnp.sum(lv[0:1, :] * lv[1:2, :], axis=1, keepdims=True)
    s2 = jnp.sum(lv[2:3, :] * lv[3:4, :], axis=1, keepdims=True)
    return jnp.exp(s1) - jnp.exp(s2) + lam_init


def _softmax_step(s, v, m, l, acc):
    m_new = jnp.maximum(m, jnp.max(s, axis=1, keepdims=True))
    p = jnp.exp(s - m_new)
    alpha = jnp.exp(m - m_new)
    l = alpha * l + jnp.sum(p, axis=1, keepdims=True)
    acc = alpha * acc + _dot(p.astype(bf16), v)
    return m_new, l, acc


def _df_finish(o, z, sn, lam_init):
    ms = jnp.mean(o * o, axis=-1, keepdims=True)
    o = (o * lax.rsqrt(ms + EPS) * sn) * (1.0 - lam_init)
    return o * _silu(z)


def _df_prompt_kernel(lamv_ref, q_ref, k_ref, v_ref, z_ref, sn_ref, o_ref, *, tq, lam_init):
    i = pl.program_id(2)
    q = q_ref[...]
    lane = lax.broadcasted_iota(jnp.int32, (tq, HEAD_W), 1)
    zero = jnp.zeros_like(q)
    q0 = jnp.where(lane < DF_HALF, q, zero)
    q1 = jnp.where(lane >= DF_HALF, q, zero)
    row = lax.broadcasted_iota(jnp.int32, (tq, tq), 0)
    col = lax.broadcasted_iota(jnp.int32, (tq, tq), 1)
    diag_mask = col <= row

    def kv(j):
        start = pl.multiple_of(j * tq, tq)
        return k_ref[pl.ds(start, tq), :], v_ref[pl.ds(start, tq), :]

    def block(j, st, mask):
        k, v = kv(j)
        s0 = _dot_nt(q0, k) * DF_SCALE
        s1 = _dot_nt(q1, k) * DF_SCALE
        if mask is not None:
            s0 = jnp.where(mask, s0, NEG_BIG)
            s1 = jnp.where(mask, s1, NEG_BIG)
        m0, l0, a0 = _softmax_step(s0, v, st[0], st[1], st[2])
        m1, l1, a1 = _softmax_step(s1, v, st[3], st[4], st[5])
        return (m0, l0, a0, m1, l1, a1)

    neg = jnp.full((tq, 1), NEG_BIG, f32)
    zl = jnp.zeros((tq, 1), f32)
    za = jnp.zeros((tq, HEAD_W), f32)
    st = block(i, (neg, zl, za, neg, zl, za), diag_mask)
    st = lax.fori_loop(0, i, lambda jj, s: block(jj, s, None), st)
    lam = _lambda_value(lamv_ref, lam_init)
    o = st[2] / st[1] - lam * (st[5] / st[4])
    o_ref[...] = _df_finish(o, z_ref[...], sn_ref[...], lam_init).astype(o_ref.dtype)


def _df_prompt(lamv, q, k, v, z, sub_norm, b, l, lam_init, tq=256):
    tq = min(tq, l)
    nq = l // tq
    return pl.pallas_call(
        functools.partial(_df_prompt_kernel, tq=tq, lam_init=lam_init),
        grid=(b, N_HEADS, nq),
        in_specs=[
            pl.BlockSpec((8, LANES), lambda bi, h, i: (0, 0)),
            pl.BlockSpec((tq, HEAD_W), lambda bi, h, i: (bi * nq + i, h)),
            pl.BlockSpec((l, HEAD_W), lambda bi, h, i: (bi, h)),
            pl.BlockSpec((l, HEAD_W), lambda bi, h, i: (bi, h)),
            pl.BlockSpec((tq, HEAD_W), lambda bi, h, i: (bi * nq + i, h)),
            pl.BlockSpec((1, HEAD_W), lambda bi, h, i: (0, 0)),
        ],
        out_specs=pl.BlockSpec((tq, HEAD_W), lambda bi, h, i: (bi * nq + i, h)),
        out_shape=jax.ShapeDtypeStruct((b * l, ATT_INNER), bf16),
        compiler_params=_cparams(("parallel", "parallel", "arbitrary")),
        name="df_prompt",
    )(lamv, q, k, v, z, sub_norm.reshape(1, HEAD_W))


PPS = 4
NEW_TOK = 16
BLK = NEW_TOK * N_HEADS
PAGE_ROWS = PAGE * N_HEADS


def _page_specs(n_pages, block):
    zeros = (0,) * (len(block) - 1)

    def spec(r):
        return pl.BlockSpec(block, lambda bi, g, pt: (pt[bi, n_pages - 1 - (g * PPS + r)],) + zeros)
    return [spec(r) for r in range(PPS)]


def _head_match(rows, cols):
    r = lax.broadcasted_iota(jnp.int32, (rows, cols), 0) % N_HEADS
    c = lax.broadcasted_iota(jnp.int32, (rows, cols), 1) % N_HEADS
    return r == c


def _sb_sample_kernel(pt_ref, q_ref, kn_ref, vn_ref, z_ref, *rest, n_new, n_steps, n_q):
    k_refs = rest[:PPS]
    v_refs = rest[PPS:2 * PPS]
    o_ref, acc, carry = rest[2 * PPS:]
    g = pl.program_id(1)
    rows = n_q * N_HEADS
    q = q_ref[0]
    r2 = lax.broadcasted_iota(jnp.int32, (BLK, BLK), 0)
    c2 = lax.broadcasted_iota(jnp.int32, (BLK, BLK), 1)
    tri_after = jnp.where(r2 > c2, 1.0, 0.0).astype(bf16)
    ones = jnp.ones((BLK, BLK), bf16)

    def process(kb, vb, mask):
        n_blk = kb.shape[0] // BLK
        z = _dot_nt(q, kb) * SB_SCALE
        sp = _softplus(z)
        u = jnp.where(mask, -sp, 0.0)
        u_hi, u_lo = _split2(u)
        stack = lambda t: jnp.concatenate([t[:, BLK * b:BLK * (b + 1)] for b in range(n_blk)], axis=0)
        us_hi, us_lo = stack(u_hi), stack(u_lo)
        after_l = _dot(us_hi, tri_after) + _dot(us_lo, tri_after)
        tot_l = _dot(us_hi, ones) + _dot(us_lo, ones)
        running = carry[...]
        afters = [None] * n_blk
        for b in reversed(range(n_blk)):
            afters[b] = after_l[rows * b:rows * (b + 1), :] + running
            running = running + tot_l[rows * b:rows * (b + 1), :]
        carry[...] = running
        after = jnp.concatenate(afters, axis=1)
        w = jnp.where(mask, jnp.exp((z - sp) + after), 0.0)
        acc[...] += _dot(w.astype(bf16), vb)

    @pl.when(g == 0)
    def _first():
        acc[...] = jnp.zeros((rows, HEAD_W), f32)
        carry[...] = jnp.zeros((rows, BLK), f32)
        qi = lax.broadcasted_iota(jnp.int32, (rows, BLK), 0) // N_HEADS
        ti = lax.broadcasted_iota(jnp.int32, (rows, BLK), 1) // N_HEADS
        process(kn_ref[0], vn_ref[0], _head_match(rows, BLK) & (ti < qi) & (ti < n_new))

    page_mask = _head_match(rows, PAGE_ROWS)
    for r in range(PPS):
        process(k_refs[r][0].reshape(PAGE_ROWS, HEAD_W).astype(bf16),
                v_refs[r][0].reshape(PAGE_ROWS, HEAD_W).astype(bf16), page_mask)

    @pl.when(g == n_steps - 1)
    def _fin():
        o_ref[0] = acc[...] * _silu(z_ref[0])


def _heads_to_rows(x):
    bs, t, _ = x.shape
    return x.reshape(bs, t * N_HEADS, HEAD_W)


def _pad_tokens(x, t_pad):
    bs, t, d = x.shape
    return jnp.concatenate([x, jnp.zeros((bs, t_pad - t, d), x.dtype)], axis=1)


def _sb_sample(q, k, v, z, cache_k, cache_v, page_table):
    bs, ls, _ = q.shape
    n_pages = page_table.shape[1]
    n_steps = n_pages // PPS
    rows = ls * N_HEADS
    qr = _heads_to_rows(q.astype(bf16))
    kn = _heads_to_rows(_pad_tokens(k.astype(bf16), NEW_TOK))
    vn = _heads_to_rows(_pad_tokens(v.astype(bf16), NEW_TOK))
    zr = _heads_to_rows(z)
    per_seq = lambda r: pl.BlockSpec((1, r, HEAD_W), lambda bi, g, pt: (bi, 0, 0))
    page_block = (1, PAGE, N_HEADS, HEAD_W)
    out = pl.pallas_call(
        functools.partial(_sb_sample_kernel, n_new=ls, n_steps=n_steps, n_q=ls),
        grid_spec=pltpu.PrefetchScalarGridSpec(
            num_scalar_prefetch=1,
            grid=(bs, n_steps),
            in_specs=[per_seq(rows), per_seq(BLK), per_seq(BLK), per_seq(rows)]
            + _page_specs(n_pages, page_block) + _page_specs(n_pages, page_block),
            out_specs=per_seq(rows),
            scratch_shapes=[pltpu.VMEM((rows, HEAD_W), f32), pltpu.VMEM((rows, BLK), f32)],
        ),
        out_shape=jax.ShapeDtypeStruct((bs, rows, HEAD_W), f32),
        compiler_params=_cparams(("parallel", "arbitrary")),
        name="sb_sample",
    )(page_table, qr, kn, vn, zr, *([cache_k] * PPS), *([cache_v] * PPS))
    return out.reshape(bs, ls, ATT_INNER)


def _df_sample_kernel(pt_ref, lamv_ref, q_ref, kn_ref, vn_ref, z_ref, sn_ref, erep_ref, *rest,
                      n_new, n_steps, n_q, lam_init):
    k_refs = rest[:PPS]
    v_refs = rest[PPS:2 * PPS]
    o_ref, acc, m_s, l_s, qbd, knp = rest[2 * PPS:]
    g = pl.program_id(1)
    half = n_q * N_HEADS
    rows = 2 * half

    def process(s, vb, smask, hmask):
        if smask is not None:
            s = jnp.where(smask, s, NEG_BIG)
        m_old = m_s[...]
        m_new = jnp.maximum(m_old, jnp.max(s, axis=1, keepdims=True))
        p = jnp.exp(s - m_new)
        alpha = jnp.exp(m_old - m_new)
        l_s[...] = alpha * l_s[...] + jnp.sum(p, axis=1, keepdims=True)
        m_s[...] = m_new
        n_cols = vb.shape[0]
        pexp = _dot(p.astype(bf16), erep_ref[:, 0:n_cols])
        pexp = jnp.where(hmask, pexp, 0.0).astype(bf16)
        acc[...] = alpha * acc[...] + _dot(pexp, vb)

    @pl.when(g == 0)
    def _first():
        acc[...] = jnp.zeros((rows, HEAD_W), f32)
        m_s[...] = jnp.full((rows, 1), NEG_BIG, f32)
        l_s[...] = jnp.zeros((rows, 1), f32)
        qq = q_ref[0]
        hrow = lax.broadcasted_iota(jnp.int32, (N_HEADS, ATT_INNER), 0)
        col = lax.broadcasted_iota(jnp.int32, (N_HEADS, ATT_INNER), 1)
        for c in range(2):
            sel = (col // HEAD_W == hrow) & ((col // DF_HALF) % 2 == c)
            for qi in range(n_q):
                r0 = (c * n_q + qi) * N_HEADS
                qbd[r0:r0 + N_HEADS, :] = jnp.where(sel, qq[qi:qi + 1, :], 0.0).astype(bf16)
        knp[...] = jnp.zeros((PAGE, ATT_INNER), bf16)
        knp[0:NEW_TOK, :] = kn_ref[0]
        s_new = _dot_nt(qbd[...], knp[...]) * DF_SCALE
        qi = (lax.broadcasted_iota(jnp.int32, (rows, PAGE), 0) // N_HEADS) % n_q
        ti = lax.broadcasted_iota(jnp.int32, (rows, PAGE), 1)
        process(s_new, vn_ref[0], (ti <= qi) & (ti < n_new), _head_match(rows, BLK))

    page_mask = _head_match(rows, PAGE_ROWS)
    for r in range(PPS):
        s = _dot(qbd[...], k_refs[r][0].astype(bf16)) * DF_SCALE
        process(s, v_refs[r][0].reshape(PAGE_ROWS, HEAD_W).astype(bf16), None, page_mask)

    @pl.when(g == n_steps - 1)
    def _fin():
        lam = _lambda_value(lamv_ref, lam_init)
        o_all = acc[...] / l_s[...]
        o = o_all[0:half, :] - lam * o_all[half:rows, :]
        o_ref[0] = _df_finish(o, z_ref[0], sn_ref[...], lam_init)


def _df_sample(lamv, q, k, v, z, sub_norm, cache_kt, cache_v, page_table, lam_init):
    bs, ls, _ = q.shape
    n_pages = page_table.shape[1]
    n_steps = n_pages // PPS
    half = ls * N_HEADS
    rows = 2 * half
    kn = _pad_tokens(k.astype(bf16), NEW_TOK)
    vn = _heads_to_rows(_pad_tokens(v.astype(bf16), NEW_TOK))
    zr = _heads_to_rows(z)
    erep = (jnp.arange(PAGE_ROWS)[None, :] // N_HEADS == jnp.arange(PAGE)[:, None]).astype(bf16)
    per_seq = lambda r, w: pl.BlockSpec((1, r, w), lambda bi, g, pt: (bi, 0, 0))
    const = lambda r, w: pl.BlockSpec((r, w), lambda bi, g, pt: (0, 0))
    out = pl.pallas_call(
        functools.partial(_df_sample_kernel, n_new=ls, n_steps=n_steps, n_q=ls, lam_init=lam_init),
        grid_spec=pltpu.PrefetchScalarGridSpec(
            num_scalar_prefetch=1,
            grid=(bs, n_steps),
            in_specs=[const(8, LANES), per_seq(ls, ATT_INNER), per_seq(NEW_TOK, ATT_INNER), per_seq(BLK, HEAD_W),
                      per_seq(half, HEAD_W), const(1, HEAD_W), const(PAGE, PAGE_ROWS)]
            + _page_specs(n_pages, (1, ATT_INNER, PAGE)) + _page_specs(n_pages, (1, PAGE, N_HEADS, HEAD_W)),
            out_specs=per_seq(half, HEAD_W),
            scratch_shapes=[pltpu.VMEM((rows, HEAD_W), f32), pltpu.VMEM((rows, 1), f32), pltpu.VMEM((rows, 1), f32),
                            pltpu.VMEM((rows, ATT_INNER), bf16), pltpu.VMEM((PAGE, ATT_INNER), bf16)],
        ),
        out_shape=jax.ShapeDtypeStruct((bs, half, HEAD_W), f32),
        compiler_params=_cparams(("parallel", "arbitrary")),
        name="df_sample",
    )(page_table, lamv, q, kn, vn, zr, sub_norm.reshape(1, HEAD_W), erep, *([cache_kt] * PPS), *([cache_v] * PPS))
    return out.reshape(bs, ls, ATT_INNER)


def _even_weights(w_in, w_out):
    o = 0
    segs = []
    for size in (M_INNER, M_CONV_DIM, M_HEADS, ATT_INNER, ATT_INNER, ATT_INNER, ATT_INNER):
        segs.append(w_in[:, o:o + size].astype(bf16))
        o += size
    segs[2] = jnp.concatenate([segs[2], jnp.zeros((D_MODEL, LANES - M_HEADS), bf16)], axis=1)
    return segs, w_out.astype(bf16)


def _even_layer(x, b, l, h0, conv0, past, norm_w, wsegs, w_out, conv_w, conv_b, dt_bias, a_log, d_skip, m_norm):
    w_zm, w_xbc, w_dt, w_zsb, w_q, w_k, w_v = wsegs
    xn = _rmsnorm(x, norm_w)
    z_m = _matmul([xn], w_zm)
    xbc = _matmul([xn], w_xbc)
    dt = _matmul([xn], w_dt)
    z_sb = _matmul([xn], w_zsb)
    xbc3 = xbc.reshape(b, l, M_CONV_DIM)
    y_m, h_new = _ssd(xbc3, z_m.reshape(b, l, M_INNER), dt.reshape(b, l, LANES), conv0, h0,
                      conv_w, conv_b, dt_bias, a_log, d_skip, m_norm)
    conv_new = jnp.concatenate([conv0, xbc3], axis=1)[:, -(M_CONV - 1):]
    if past is None:
        q_b = _matmul([xn], w_q, out_kinds=(bf16,))
        k, k_b = _matmul([xn], w_k, out_kinds=(f32, bf16))
        v, v_b = _matmul([xn], w_v, out_kinds=(f32, bf16))
        y_sb = _sb_prompt(q_b, k_b, v_b, z_sb, b, l)
    else:
        q = _matmul([xn], w_q)
        k = _matmul([xn], w_k)
        v = _matmul([xn], w_v)
        r3 = lambda t: t.reshape(b, l, ATT_INNER)
        y_sb = _sb_sample(r3(q), r3(k), r3(v), r3(z_sb), *past).reshape(b * l, ATT_INNER).astype(bf16)
    x_new = _matmul([y_m.reshape(b * l, M_INNER), y_sb], w_out, res=x, tn=512)
    return x_new, h_new, conv_new, k, v


def _odd_layer(x, b, l, past, lam_init, norm_w, wsegs, w_out, lamv, sub_norm, norm_f):
    w_z, w_q, w_k, w_v = wsegs
    xn = _rmsnorm(x, norm_w)
    z = _matmul([xn], w_z)
    if past is None:
        q_b = _matmul([xn], w_q, out_kinds=(bf16,))
        k, k_b = _matmul([xn], w_k, out_kinds=(f32, bf16))
        v, v_b = _matmul([xn], w_v, out_kinds=(f32, bf16))
        y = _df_prompt(lamv, q_b, k_b, v_b, z, sub_norm, b, l, lam_init)
    else:
        q = _matmul([xn], w_q)
        k = _matmul([xn], w_k)
        v = _matmul([xn], w_v)
        r3 = lambda t: t.reshape(b, l, ATT_INNER)
        y = _df_sample(lamv, r3(q), r3(k), r3(v), r3(z), sub_norm, *past, lam_init)
        y = y.reshape(b * l, ATT_INNER).astype(bf16)
    out = _matmul([y], w_out, res=x, norm_w=norm_f, tm=512)
    return out, k, v


def kernel(x_prompt, x_sample, state_ssm, state_conv, cache_sb_k, cache_sb_v, cache_df_k, cache_df_v, page_table,
           norm_even, w_in_even, conv_w, conv_b, dt_bias, a_log, d_skip, m_norm, w_out_even,
           norm_odd, w_in_odd, lambda_q1, lambda_k1, lambda_q2, lambda_k2, df_norm, w_out_odd, norm_f):
    bp, lp, d = x_prompt.shape
    bs, ls, _ = x_sample.shape
    n_pool = cache_sb_k.shape[1]
    xp = x_prompt.reshape(bp * lp, d)
    xs = x_sample.reshape(bs * ls, d)

    wsegs, w_out = _even_weights(w_in_even[0], w_out_even[0])
    ew = (norm_even[0], wsegs, w_out, conv_w[0], conv_b[0], dt_bias[0], a_log[0], d_skip[0], m_norm[0])
    h0 = jnp.zeros((bp,) + state_ssm.shape[2:], f32)
    c0 = jnp.zeros((bp,) + state_conv.shape[2:], f32)
    xp, ssm_p, conv_p, sbk_p, sbv_p = _even_layer(xp, bp, lp, h0, c0, None, *ew)
    past = (cache_sb_k[0], cache_sb_v[0], page_table)
    xs, ssm_s, conv_s, sbk_s, sbv_s = _even_layer(xs, bs, ls, state_ssm[0], state_conv[0], past, *ew)

    lam_init = 0.8 - 0.6 * math.exp(-0.3 * 1)
    w_in = w_in_odd[0]
    osegs = [w_in[:, ATT_INNER * s:ATT_INNER * (s + 1)].astype(bf16) for s in range(4)]
    lamv = jnp.zeros((8, LANES), f32)
    lamv = lamv.at[0, :DF_HALF].set(lambda_q1[0]).at[1, :DF_HALF].set(lambda_k1[0])
    lamv = lamv.at[2, :DF_HALF].set(lambda_q2[0]).at[3, :DF_HALF].set(lambda_k2[0])
    ow = (norm_odd[0], osegs, w_out_odd[0].astype(bf16), lamv, df_norm[0], norm_f)
    y_prompt, dfk_p, dfv_p = _odd_layer(xp, bp, lp, None, lam_init, *ow)
    cache_kt = jnp.transpose(cache_df_k[0], (0, 2, 3, 4, 1)).reshape(n_pool, ATT_INNER, PAGE)
    past = (cache_kt, cache_df_v[0], page_table)
    y_sample, dfk_s, dfv_s = _odd_layer(xs, bs, ls, past, lam_init, *ow)

    hd = (N_HEADS, HEAD_W)
    hd2 = (N_HEADS, 2, DF_HALF)
    return (y_prompt.reshape(bp, lp, d), y_sample.reshape(bs, ls, d),
            ssm_p[None], conv_p[None],
            sbk_p.reshape((1, bp, lp) + hd), sbv_p.reshape((1, bp, lp) + hd),
            dfk_p.reshape((1, bp, lp) + hd2), dfv_p.reshape((1, bp, lp) + hd),
            ssm_s[None], conv_s[None],
            sbk_s.reshape((1, bs, ls) + hd), sbv_s.reshape((1, bs, ls) + hd),
            dfk_s.reshape((1, bs, ls) + hd2), dfv_s.reshape((1, bs, ls) + hd))
```

```python
import functools
import math

import jax
import jax.numpy as jnp
from jax import lax
from jax.experimental import pallas as pl
from jax.experimental.pallas import tpu as pltpu

f32 = jnp.float32
bf16 = jnp.bfloat16

D_MODEL = 2048
EPS = 1e-6
M_HEADS = 32
M_HEAD_DIM = 64
M_INNER = M_HEADS * M_HEAD_DIM
M_GROUPS = 4
M_STATE = 128
M_CONV = 4
M_CONV_DIM = M_INNER + 2 * M_GROUPS * M_STATE
SSD_CHUNK = 128
N_HEADS = 16
HEAD_W = 128
DF_HALF = 64
ATT_INNER = N_HEADS * HEAD_W
SB_SCALE = HEAD_W ** -0.5
DF_SCALE = DF_HALF ** -0.5
PAGE = 128
NEG_BIG = -1e30
SB_DEAD = -110.0

LANES = 128
VMEM_LIMIT = 56 * 1024 * 1024


def _cparams(sem):
    return pltpu.CompilerParams(dimension_semantics=sem, vmem_limit_bytes=VMEM_LIMIT)


def _dot(a, b):
    return jnp.dot(a, b, preferred_element_type=f32)


def _dot_nt(a, b):
    return lax.dot_general(a, b, (((1,), (1,)), ((), ())), preferred_element_type=f32)


def _softplus(x):
    return jnp.maximum(x, 0.0) + jnp.log1p(jnp.exp(-jnp.abs(x)))


def _silu(x):
    return x * (1.0 / (1.0 + jnp.exp(-x)))


def _split2(x):
    hi = x.astype(bf16)
    lo = (x - hi.astype(f32)).astype(bf16)
    return hi, lo


def _split3(x):
    hi = x.astype(bf16)
    r = x - hi.astype(f32)
    mid = r.astype(bf16)
    lo = (r - mid.astype(f32)).astype(bf16)
    return hi, mid, lo


def _rmsnorm_kernel(x_ref, w_ref, o_ref):
    x = x_ref[...]
    ms = jnp.mean(x * x, axis=-1, keepdims=True)
    o_ref[...] = (x * lax.rsqrt(ms + EPS) * w_ref[...]).astype(o_ref.dtype)


def _rmsnorm(x, w, tm=512):
    m, d = x.shape
    tm = min(tm, m)
    return pl.pallas_call(
        _rmsnorm_kernel,
        grid=(m // tm,),
        in_specs=[pl.BlockSpec((tm, d), lambda i: (i, 0)), pl.BlockSpec((1, d), lambda i: (0, 0))],
        out_specs=pl.BlockSpec((tm, d), lambda i: (i, 0)),
        out_shape=jax.ShapeDtypeStruct((m, d), bf16),
        compiler_params=_cparams(("parallel",)),
        name="rmsnorm",
    )(x, w.reshape(1, d))


def _matmul_kernel(*refs, n_in, has_res, has_norm, out_kinds):
    xs = refs[:n_in]
    w_ref = refs[n_in]
    pos = n_in + 1
    res_ref = nw_ref = None
    if has_res:
        res_ref = refs[pos]
        pos += 1
    if has_norm:
        nw_ref = refs[pos]
        pos += 1
    outs = refs[pos:]
    acc = None
    k0 = 0
    for x_ref in xs:
        kk = x_ref.shape[1]
        part = _dot(x_ref[...], w_ref[k0:k0 + kk, :])
        acc = part if acc is None else acc + part
        k0 += kk
    if has_res:
        acc = acc + res_ref[...]
    if has_norm:
        ms = jnp.mean(acc * acc, axis=-1, keepdims=True)
        acc = acc * lax.rsqrt(ms + EPS) * nw_ref[...]
    for o_ref, kind in zip(outs, out_kinds):
        o_ref[...] = acc.astype(kind)


def _matmul(xs, w, res=None, norm_w=None, out_kinds=(f32,), tm=1024, tn=1024):
    m = xs[0].shape[0]
    k, n = w.shape
    tm = min(tm, m)
    tn = min(tn, n)
    if norm_w is not None:
        tn = n
    in_specs = [pl.BlockSpec((tm, x.shape[1]), lambda i, j: (i, 0)) for x in xs]
    in_specs.append(pl.BlockSpec((k, tn), lambda i, j: (0, j)))
    args = list(xs) + [w]
    if res is not None:
        in_specs.append(pl.BlockSpec((tm, tn), lambda i, j: (i, j)))
        args.append(res)
    if norm_w is not None:
        in_specs.append(pl.BlockSpec((1, tn), lambda i, j: (0, j)))
        args.append(norm_w.reshape(1, n))
    outs = pl.pallas_call(
        functools.partial(_matmul_kernel, n_in=len(xs), has_res=res is not None,
                          has_norm=norm_w is not None, out_kinds=out_kinds),
        grid=(m // tm, n // tn),
        in_specs=in_specs,
        out_specs=[pl.BlockSpec((tm, tn), lambda i, j: (i, j)) for _ in out_kinds],
        out_shape=[jax.ShapeDtypeStruct((m, n), kd) for kd in out_kinds],
        compiler_params=_cparams(("parallel", "arbitrary")),
        name="matmul",
    )(*args)
    return outs[0] if len(out_kinds) == 1 else tuple(outs)


def _ssd_kernel(xbc_ref, z_ref, dt_ref, conv0_ref, h0_ref, cw_ref, cb_ref, dtb_ref, alog_ref,
                dskip_ref, mnorm_ref, e64_ref, e128_ref,
                y_ref, hout_ref, ubuf, xc, state_t, ybuf, dtbuf, *, tv, nc):
    T = SSD_CHUNK
    c = pl.program_id(1)

    @pl.when(c == 0)
    def _init():
        ubuf[0:8, :] = conv0_ref[0]
        for j in range(M_HEADS // 2):
            blk = jnp.concatenate([h0_ref[0, 2 * j], h0_ref[0, 2 * j + 1]], axis=0)
            state_t[:, LANES * j:LANES * (j + 1)] = blk.T

    @pl.when(c > 0)
    def _shift():
        ubuf[0:8, :] = ubuf[T:T + 8, :]

    if tv < T:
        ubuf[8:8 + T, :] = jnp.zeros((T, M_CONV_DIM), f32)
        ubuf[8:8 + tv, :] = xbc_ref[0]
        dtbuf[...] = jnp.zeros((T, LANES), f32)
        dtbuf[0:tv, :] = dt_ref[0]
    else:
        ubuf[8:8 + T, :] = xbc_ref[0]
        dtbuf[...] = dt_ref[0]

    cw_chunk = 512
    for cc in range(M_CONV_DIM // cw_chunk):
        cs = slice(cc * cw_chunk, (cc + 1) * cw_chunk)
        acc = cb_ref[:, cs] + cw_ref[0:1, cs] * ubuf[5:5 + T, cs]
        acc = acc + cw_ref[1:2, cs] * ubuf[6:6 + T, cs]
        acc = acc + cw_ref[2:3, cs] * ubuf[7:7 + T, cs]
        acc = acc + cw_ref[3:4, cs] * ubuf[8:8 + T, cs]
        xc[:, cs] = _silu(acc)

    row = lax.broadcasted_iota(jnp.int32, (T, T), 0)
    col = lax.broadcasted_iota(jnp.int32, (T, T), 1)
    tril = col <= row
    tril_b = jnp.where(tril, 1.0, 0.0).astype(bf16)

    dt = _softplus(dtbuf[...] + dtb_ref[...])
    if tv < T:
        dt = jnp.where(row < tv, dt, 0.0)
    a = dt * (-jnp.exp(alog_ref[...]))
    a_hi, a_mid, a_lo = _split3(a)
    a_cs = _dot(tril_b, a_hi) + _dot(tril_b, a_mid) + _dot(tril_b, a_lo)
    a_cs_t = a_cs.T
    acs3 = jnp.concatenate(_split3(a_cs), axis=1)
    dt3 = jnp.concatenate(_split3(dt), axis=1)
    a_full = _dot(acs3, e64_ref[...])
    dt_full = _dot(dt3, e64_ref[...])
    alast_full = a_full[T - 1:T, :]
    ea_full = jnp.exp(a_full)
    ds_full = jnp.exp(alast_full - a_full)
    eal_full = jnp.exp(alast_full)

    lane = lax.broadcasted_iota(jnp.int32, (T, LANES), 1)
    lo_half = lane < M_HEAD_DIM

    for g in range(M_GROUPS):
        gs = slice(512 * g, 512 * (g + 1))
        bg = xc[:, M_INNER + M_STATE * g:M_INNER + M_STATE * (g + 1)]
        cg = xc[:, M_INNER + M_GROUPS * M_STATE + M_STATE * g:M_INNER + M_GROUPS * M_STATE + M_STATE * (g + 1)]
        bg_b = bg.astype(bf16)
        cg_b = cg.astype(bf16)
        cb = _dot_nt(cg_b, bg_b)
        a_wide = _dot(acs3, e128_ref[:, 1024 * g:1024 * (g + 1)])
        xs_g = xc[:, gs] * dt_full[:, gs]
        y_off = _dot(cg_b, state_t[:, gs].astype(bf16)) * ea_full[:, gs]
        ybuf[:, gs] = y_off + dskip_ref[:, gs] * xc[:, gs]
        for jj in range(4):
            j = 4 * g + jj
            xs_pair = xs_g[:, LANES * jj:LANES * (jj + 1)]
            yd = None
            for e in range(2):
                h = 2 * j + e
                hh = 2 * jj + e
                diff = a_wide[:, LANES * hh:LANES * (hh + 1)] - a_cs_t[h:h + 1, :]
                lmat = jnp.exp(jnp.where(tril, diff, -jnp.inf))
                mh = (cb * lmat).astype(bf16)
                keep = lo_half if e == 0 else jnp.logical_not(lo_half)
                xs_h = jnp.where(keep, xs_pair, 0.0).astype(bf16)
                part = _dot(mh, xs_h)
                yd = part if yd is None else yd + part
            ybuf[:, LANES * j:LANES * (j + 1)] += yd
        upd = _dot(bg.T.astype(bf16), (xs_g * ds_full[:, gs]).astype(bf16))
        state_t[:, gs] = state_t[:, gs] * eal_full[:, gs] + upd

    for g in range(M_GROUPS):
        gs = slice(512 * g, 512 * (g + 1))
        if tv < T:
            zg = z_ref[0][:, gs]
            gz = ybuf[0:tv, gs] * _silu(zg)
        else:
            gz = ybuf[:, gs] * _silu(z_ref[0][:, gs])
        ms = jnp.mean(gz * gz, axis=-1, keepdims=True)
        y_ref[0, :, gs] = (gz * lax.rsqrt(ms + EPS) * mnorm_ref[:, gs]).astype(y_ref.dtype)

    @pl.when(c == nc - 1)
    def _fin():
        for j in range(M_HEADS // 2):
            tr = state_t[:, LANES * j:LANES * (j + 1)].T
            hout_ref[0, 2 * j] = tr[0:M_HEAD_DIM]
            hout_ref[0, 2 * j + 1] = tr[M_HEAD_DIM:2 * M_HEAD_DIM]


def _ssd(xbc, z, dt, conv0, h0, conv_w, conv_b, dt_bias, a_log, d_skip, m_norm):
    b, l, _ = xbc.shape
    T = SSD_CHUNK
    tv = T if l % T == 0 else l
    nc = l // tv
    conv0p = jnp.concatenate([jnp.zeros((b, 5, M_CONV_DIM), f32), conv0], axis=1)
    cwp = jnp.concatenate([conv_w, jnp.zeros((8 - M_CONV, M_CONV_DIM), f32)], axis=0)
    pad = LANES - M_HEADS

    def padh(v):
        return jnp.concatenate([v, jnp.zeros((pad,), f32)]).reshape(1, LANES)

    hid = jnp.arange(LANES)[:, None]
    e64 = (jnp.arange(M_INNER)[None, :] // M_HEAD_DIM == hid).astype(bf16)
    e64 = jnp.concatenate([e64, e64, e64], axis=0)
    e128 = (jnp.arange(M_HEADS * LANES)[None, :] // LANES == hid).astype(bf16)
    e128 = jnp.concatenate([e128, e128, e128], axis=0)
    dskip_full = jnp.repeat(d_skip, M_HEAD_DIM).reshape(1, M_INNER)

    const = lambda shape: pl.BlockSpec(shape, lambda bi, ci: (0,) * len(shape))
    y, hout = pl.pallas_call(
        functools.partial(_ssd_kernel, tv=tv, nc=nc),
        grid=(b, nc),
        in_specs=[
            pl.BlockSpec((1, tv, M_CONV_DIM), lambda bi, ci: (bi, ci, 0)),
            pl.BlockSpec((1, tv, M_INNER), lambda bi, ci: (bi, ci, 0)),
            pl.BlockSpec((1, tv, LANES), lambda bi, ci: (bi, ci, 0)),
            pl.BlockSpec((1, 8, M_CONV_DIM), lambda bi, ci: (bi, 0, 0)),
            pl.BlockSpec((1, M_HEADS, M_HEAD_DIM, M_STATE), lambda bi, ci: (bi, 0, 0, 0)),
            const((8, M_CONV_DIM)), const((1, M_CONV_DIM)), const((1, LANES)), const((1, LANES)),
            const((1, M_INNER)), const((1, M_INNER)), const((3 * LANES, M_INNER)),
            const((3 * LANES, M_HEADS * LANES)),
        ],
        out_specs=[
            pl.BlockSpec((1, tv, M_INNER), lambda bi, ci: (bi, ci, 0)),
            pl.BlockSpec((1, M_HEADS, M_HEAD_DIM, M_STATE), lambda bi, ci: (bi, 0, 0, 0)),
        ],
        out_shape=[jax.ShapeDtypeStruct((b, l, M_INNER), bf16),
                   jax.ShapeDtypeStruct((b, M_HEADS, M_HEAD_DIM, M_STATE), f32)],
        scratch_shapes=[
            pltpu.VMEM((T + 8, M_CONV_DIM), f32),
            pltpu.VMEM((T, M_CONV_DIM), f32),
            pltpu.VMEM((M_STATE, M_INNER), f32),
            pltpu.VMEM((T, M_INNER), f32),
            pltpu.VMEM((T, LANES), f32),
        ],
        compiler_params=_cparams(("parallel", "arbitrary")),
        name="ssd",
    )(xbc, z, dt, conv0p, h0, cwp, conv_b.reshape(1, M_CONV_DIM), padh(dt_bias), padh(a_log),
      dskip_full, m_norm.reshape(1, M_INNER), e64, e128)
    return y, hout


def _sb_block(q, k, v, tri_after, carry, acc, mask):
    z = _dot_nt(q, k) * SB_SCALE
    sp = _softplus(z)
    u = -sp
    if mask is not None:
        u = jnp.where(mask, u, 0.0)
    u_hi, u_lo = _split2(u)
    after = _dot(u_hi, tri_after) + _dot(u_lo, tri_after)
    w = jnp.exp((z - sp) + after + carry)
    if mask is not None:
        w = jnp.where(mask, w, 0.0)
    acc = acc + _dot(w.astype(bf16), v)
    carry = carry + jnp.sum(u, axis=1, keepdims=True)
    return carry, acc


def _sb_prompt_kernel(q_ref, k_ref, v_ref, z_ref, o_ref, *, tq):
    i = pl.program_id(2)
    q = q_ref[...]
    row = lax.broadcasted_iota(jnp.int32, (tq, tq), 0)
    col = lax.broadcasted_iota(jnp.int32, (tq, tq), 1)
    tri_after = jnp.where(row > col, 1.0, 0.0).astype(bf16)
    diag_mask = col < row

    def kv(j):
        start = pl.multiple_of(j * tq, tq)
        return k_ref[pl.ds(start, tq), :], v_ref[pl.ds(start, tq), :]

    k, v = kv(i)
    carry, acc = _sb_block(q, k, v, tri_after, jnp.zeros((tq, 1), f32), jnp.zeros((tq, HEAD_W), f32), diag_mask)

    def cond(st):
        return jnp.logical_and(st[0] < i, jnp.max(st[1]) > SB_DEAD)

    def body(st):
        jj, carry, acc = st
        k, v = kv(i - 1 - jj)
        carry, acc = _sb_block(q, k, v, tri_after, carry, acc, None)
        return jj + 1, carry, acc

    _, carry, acc = lax.while_loop(cond, body, (jnp.int32(0), carry, acc))
    o_ref[...] = (acc * _silu(z_ref[...])).astype(o_ref.dtype)


def _sb_prompt(q, k, v, z, b, l, tq=256):
    tq = min(tq, l)
    nq = l // tq
    return pl.pallas_call(
        functools.partial(_sb_prompt_kernel, tq=tq),
        grid=(b, N_HEADS, nq),
        in_specs=[
            pl.BlockSpec((tq, HEAD_W), lambda bi, h, i: (bi * nq + i, h)),
            pl.BlockSpec((l, HEAD_W), lambda bi, h, i: (bi, h)),
            pl.BlockSpec((l, HEAD_W), lambda bi, h, i: (bi, h)),
            pl.BlockSpec((tq, HEAD_W), lambda bi, h, i: (bi * nq + i, h)),
        ],
        out_specs=pl.BlockSpec((tq, HEAD_W), lambda bi, h, i: (bi * nq + i, h)),
        out_shape=jax.ShapeDtypeStruct((b * l, ATT_INNER), bf16),
        compiler_params=_cparams(("parallel", "parallel", "arbitrary")),
        name="sb_prompt",
    )(q, k, v, z)


def _lambda_value(lamv_ref, lam_init):
    lv = lamv_ref[...]
    s1 = jnp.sum(lv[0:1, :] * lv[1:2, :], axis=1, keepdims=True)
    s2 = jnp.sum(lv[2:3, :] * lv[3:4, :], axis=1, keepdims=True)
    return jnp.exp(s1) - jnp.exp(s2) + lam_init


def _softmax_step(s, v, m, l, acc):
    m_new = jnp.maximum(m, jnp.max(s, axis=1, keepdims=True))
    p = jnp.exp(s - m_new)
    alpha = jnp.exp(m - m_new)
    l = alpha * l + jnp.sum(p, axis=1, keepdims=True)
    acc = alpha * acc + _dot(p.astype(bf16), v)
    return m_new, l, acc


def _df_finish(o, z, sn, lam_init):
    ms = jnp.mean(o * o, axis=-1, keepdims=True)
    o = (o * lax.rsqrt(ms + EPS) * sn) * (1.0 - lam_init)
    return o * _silu(z)


def _df_prompt_kernel(lamv_ref, q_ref, k_ref, v_ref, z_ref, sn_ref, o_ref, *, tq, lam_init):
    i = pl.program_id(2)
    q = q_ref[...]
    lane = lax.broadcasted_iota(jnp.int32, (tq, HEAD_W), 1)
    zero = jnp.zeros_like(q)
    q0 = jnp.where(lane < DF_HALF, q, zero)
    q1 = jnp.where(lane >= DF_HALF, q, zero)
    row = lax.broadcasted_iota(jnp.int32, (tq, tq), 0)
    col = lax.broadcasted_iota(jnp.int32, (tq, tq), 1)
    diag_mask = col <= row

    def kv(j):
        start = pl.multiple_of(j * tq, tq)
        return k_ref[pl.ds(start, tq), :], v_ref[pl.ds(start, tq), :]

    def block(j, st, mask):
        k, v = kv(j)
        s0 = _dot_nt(q0, k) * DF_SCALE
        s1 = _dot_nt(q1, k) * DF_SCALE
        if mask is not None:
            s0 = jnp.where(mask, s0, NEG_BIG)
            s1 = jnp.where(mask, s1, NEG_BIG)
        m0, l0, a0 = _softmax_step(s0, v, st[0], st[1], st[2])
        m1, l1, a1 = _softmax_step(s1, v, st[3], st[4], st[5])
        return (m0, l0, a0, m1, l1, a1)

    neg = jnp.full((tq, 1), NEG_BIG, f32)
    zl = jnp.zeros((tq, 1), f32)
    za = jnp.zeros((tq, HEAD_W), f32)
    st = block(i, (neg, zl, za, neg, zl, za), diag_mask)
    st = lax.fori_loop(0, i, lambda jj, s: block(jj, s, None), st)
    lam = _lambda_value(lamv_ref, lam_init)
    o = st[2] / st[1] - lam * (st[5] / st[4])
    o_ref[...] = _df_finish(o, z_ref[...], sn_ref[...], lam_init).astype(o_ref.dtype)


DF_PROMPT_TQ = 512


def _df_prompt(lamv, q, k, v, z, sub_norm, b, l, lam_init):
    tq = min(DF_PROMPT_TQ, l)
    nq = l // tq
    return pl.pallas_call(
        functools.partial(_df_prompt_kernel, tq=tq, lam_init=lam_init),
        grid=(b, N_HEADS, nq),
        in_specs=[
            pl.BlockSpec((8, LANES), lambda bi, h, i: (0, 0)),
            pl.BlockSpec((tq, HEAD_W), lambda bi, h, i: (bi * nq + i, h)),
            pl.BlockSpec((l, HEAD_W), lambda bi, h, i: (bi, h)),
            pl.BlockSpec((l, HEAD_W), lambda bi, h, i: (bi, h)),
            pl.BlockSpec((tq, HEAD_W), lambda bi, h, i: (bi * nq + i, h)),
            pl.BlockSpec((1, HEAD_W), lambda bi, h, i: (0, 0)),
        ],
        out_specs=pl.BlockSpec((tq, HEAD_W), lambda bi, h, i: (bi * nq + i, h)),
        out_shape=jax.ShapeDtypeStruct((b * l, ATT_INNER), bf16),
        compiler_params=_cparams(("parallel", "parallel", "arbitrary")),
        name="df_prompt",
    )(lamv, q, k, v, z, sub_norm.reshape(1, HEAD_W))


PPS = 4
NEW_TOK = 16
BLK = NEW_TOK * N_HEADS
PAGE_ROWS = PAGE * N_HEADS


def _page_specs(n_pages, block):
    zeros = (0,) * (len(block) - 1)

    def spec(r):
        return pl.BlockSpec(block, lambda bi, g, pt: (pt[bi, n_pages - 1 - (g * PPS + r)],) + zeros)
    return [spec(r) for r in range(PPS)]


def _head_match(rows, cols):
    r = lax.broadcasted_iota(jnp.int32, (rows, cols), 0) % N_HEADS
    c = lax.broadcasted_iota(jnp.int32, (rows, cols), 1) % N_HEADS
    return r == c


def _sb_sample_kernel(pt_ref, q_ref, kn_ref, vn_ref, z_ref, *rest, n_new, n_steps, n_q):
    k_refs = rest[:PPS]
    v_refs = rest[PPS:2 * PPS]
    o_ref, acc, carry = rest[2 * PPS:]
    g = pl.program_id(1)
    rows = n_q * N_HEADS
    q = q_ref[0]
    r2 = lax.broadcasted_iota(jnp.int32, (BLK, BLK), 0)
    c2 = lax.broadcasted_iota(jnp.int32, (BLK, BLK), 1)
    tri_after = jnp.where(r2 > c2, 1.0, 0.0).astype(bf16)
    ones = jnp.ones((BLK, BLK), bf16)

    def process(kb, vb, mask):
        n_blk = kb.shape[0] // BLK
        z = _dot_nt(q, kb) * SB_SCALE
        sp = _softplus(z)
        u = jnp.where(mask, -sp, 0.0)
        u_hi, u_lo = _split2(u)
        stack = lambda t: jnp.concatenate([t[:, BLK * b:BLK * (b + 1)] for b in range(n_blk)], axis=0)
        us_hi, us_lo = stack(u_hi), stack(u_lo)
        after_l = _dot(us_hi, tri_after) + _dot(us_lo, tri_after)
        tot_l = _dot(us_hi, ones) + _dot(us_lo, ones)
        running = carry[...]
        afters = [None] * n_blk
        for b in reversed(range(n_blk)):
            afters[b] = after_l[rows * b:rows * (b + 1), :] + running
            running = running + tot_l[rows * b:rows * (b + 1), :]
        carry[...] = running
        after = jnp.concatenate(afters, axis=1)
        w = jnp.where(mask, jnp.exp((z - sp) + after), 0.0)
        acc[...] += _dot(w.astype(bf16), vb)

    @pl.when(g == 0)
    def _first():
        acc[...] = jnp.zeros((rows, HEAD_W), f32)
        carry[...] = jnp.zeros((rows, BLK), f32)
        qi = lax.broadcasted_iota(jnp.int32, (rows, BLK), 0) // N_HEADS
        ti = lax.broadcasted_iota(jnp.int32, (rows, BLK), 1) // N_HEADS
        process(kn_ref[0], vn_ref[0], _head_match(rows, BLK) & (ti < qi) & (ti < n_new))

    for r in range(PPS):
        @pl.when(jnp.max(carry[...]) > SB_DEAD)
        def _page():
            process(k_refs[r][0].reshape(PAGE_ROWS, HEAD_W).astype(bf16),
                    v_refs[r][0].reshape(PAGE_ROWS, HEAD_W).astype(bf16), _head_match(rows, PAGE_ROWS))

    @pl.when(g == n_steps - 1)
    def _fin():
        o_ref[0] = acc[...] * _silu(z_ref[0])


def _heads_to_rows(x):
    bs, t, _ = x.shape
    return x.reshape(bs, t * N_HEADS, HEAD_W)


def _pad_tokens(x, t_pad):
    bs, t, d = x.shape
    return jnp.concatenate([x, jnp.zeros((bs, t_pad - t, d), x.dtype)], axis=1)


def _sb_sample(q, k, v, z, cache_k, cache_v, page_table):
    bs, ls, _ = q.shape
    n_pages = page_table.shape[1]
    n_steps = n_pages // PPS
    rows = ls * N_HEADS
    qr = _heads_to_rows(q.astype(bf16))
    kn = _heads_to_rows(_pad_tokens(k.astype(bf16), NEW_TOK))
    vn = _heads_to_rows(_pad_tokens(v.astype(bf16), NEW_TOK))
    zr = _heads_to_rows(z)
    per_seq = lambda r: pl.BlockSpec((1, r, HEAD_W), lambda bi, g, pt: (bi, 0, 0))
    page_block = (1, PAGE, N_HEADS, HEAD_W)
    out = pl.pallas_call(
        functools.partial(_sb_sample_kernel, n_new=ls, n_steps=n_steps, n_q=ls),
        grid_spec=pltpu.PrefetchScalarGridSpec(
            num_scalar_prefetch=1,
            grid=(bs, n_steps),
            in_specs=[per_seq(rows), per_seq(BLK), per_seq(BLK), per_seq(rows)]
            + _page_specs(n_pages, page_block) + _page_specs(n_pages, page_block),
            out_specs=per_seq(rows),
            scratch_shapes=[pltpu.VMEM((rows, HEAD_W), f32), pltpu.VMEM((rows, BLK), f32)],
        ),
        out_shape=jax.ShapeDtypeStruct((bs, rows, HEAD_W), f32),
        compiler_params=_cparams(("parallel", "arbitrary")),
        name="sb_sample",
    )(page_table, qr, kn, vn, zr, *([cache_k] * PPS), *([cache_v] * PPS))
    return out.reshape(bs, ls, ATT_INNER)


def _df_sample_kernel(pt_ref, lamv_ref, q_ref, kn_ref, vn_ref, z_ref, sn_ref, erep_ref, *rest,
                      n_new, n_steps, n_q, lam_init):
    k_refs = rest[:PPS]
    v_refs = rest[PPS:2 * PPS]
    o_ref, acc, m_s, l_s, qbd, knp = rest[2 * PPS:]
    g = pl.program_id(1)
    half = n_q * N_HEADS
    rows = 2 * half

    def process(s, vb, smask, hmask):
        if smask is not None:
            s = jnp.where(smask, s, NEG_BIG)
        m_old = m_s[...]
        m_new = jnp.maximum(m_old, jnp.max(s, axis=1, keepdims=True))
        p = jnp.exp(s - m_new)
        alpha = jnp.exp(m_old - m_new)
        l_s[...] = alpha * l_s[...] + jnp.sum(p, axis=1, keepdims=True)
        m_s[...] = m_new
        n_cols = vb.shape[0]
        pexp = _dot(p.astype(bf16), erep_ref[:, 0:n_cols])
        pexp = jnp.where(hmask, pexp, 0.0).astype(bf16)
        acc[...] = alpha * acc[...] + _dot(pexp, vb)

    @pl.when(g == 0)
    def _first():
        acc[...] = jnp.zeros((rows, HEAD_W), f32)
        m_s[...] = jnp.full((rows, 1), NEG_BIG, f32)
        l_s[...] = jnp.zeros((rows, 1), f32)
        qq = q_ref[0]
        hrow = lax.broadcasted_iota(jnp.int32, (N_HEADS, ATT_INNER), 0)
        col = lax.broadcasted_iota(jnp.int32, (N_HEADS, ATT_INNER), 1)
        for c in range(2):
            sel = (col // HEAD_W == hrow) & ((col // DF_HALF) % 2 == c)
            for qi in range(n_q):
                r0 = (c * n_q + qi) * N_HEADS
                qbd[r0:r0 + N_HEADS, :] = jnp.where(sel, qq[qi:qi + 1, :], 0.0).astype(bf16)
        knp[...] = jnp.zeros((PAGE, ATT_INNER), bf16)
        knp[0:NEW_TOK, :] = kn_ref[0]
        s_new = _dot_nt(qbd[...], knp[...]) * DF_SCALE
        qi = (lax.broadcasted_iota(jnp.int32, (rows, PAGE), 0) // N_HEADS) % n_q
        ti = lax.broadcasted_iota(jnp.int32, (rows, PAGE), 1)
        process(s_new, vn_ref[0], (ti <= qi) & (ti < n_new), _head_match(rows, BLK))

    page_mask = _head_match(rows, PAGE_ROWS)
    for r in range(PPS):
        s = _dot(qbd[...], k_refs[r][0].astype(bf16)) * DF_SCALE
        process(s, v_refs[r][0].reshape(PAGE_ROWS, HEAD_W).astype(bf16), None, page_mask)

    @pl.when(g == n_steps - 1)
    def _fin():
        lam = _lambda_value(lamv_ref, lam_init)
        o_all = acc[...] / l_s[...]
        o = o_all[0:half, :] - lam * o_all[half:rows, :]
        o_ref[0] = _df_finish(o, z_ref[0], sn_ref[...], lam_init)


def _df_sample(lamv, q, k, v, z, sub_norm, cache_kt, cache_v, page_table, lam_init):
    bs, ls, _ = q.shape
    n_pages = page_table.shape[1]
    n_steps = n_pages // PPS
    half = ls * N_HEADS
    rows = 2 * half
    kn = _pad_tokens(k.astype(bf16), NEW_TOK)
    vn = _heads_to_rows(_pad_tokens(v.astype(bf16), NEW_TOK))
    zr = _heads_to_rows(z)
    erep = (jnp.arange(PAGE_ROWS)[None, :] // N_HEADS == jnp.arange(PAGE)[:, None]).astype(bf16)
    per_seq = lambda r, w: pl.BlockSpec((1, r, w), lambda bi, g, pt: (bi, 0, 0))
    const = lambda r, w: pl.BlockSpec((r, w), lambda bi, g, pt: (0, 0))
    out = pl.pallas_call(
        functools.partial(_df_sample_kernel, n_new=ls, n_steps=n_steps, n_q=ls, lam_init=lam_init),
        grid_spec=pltpu.PrefetchScalarGridSpec(
            num_scalar_prefetch=1,
            grid=(bs, n_steps),
            in_specs=[const(8, LANES), per_seq(ls, ATT_INNER), per_seq(NEW_TOK, ATT_INNER), per_seq(BLK, HEAD_W),
                      per_seq(half, HEAD_W), const(1, HEAD_W), const(PAGE, PAGE_ROWS)]
            + _page_specs(n_pages, (1, ATT_INNER, PAGE)) + _page_specs(n_pages, (1, PAGE, N_HEADS, HEAD_W)),
            out_specs=per_seq(half, HEAD_W),
            scratch_shapes=[pltpu.VMEM((rows, HEAD_W), f32), pltpu.VMEM((rows, 1), f32), pltpu.VMEM((rows, 1), f32),
                            pltpu.VMEM((rows, ATT_INNER), bf16), pltpu.VMEM((PAGE, ATT_INNER), bf16)],
        ),
        out_shape=jax.ShapeDtypeStruct((bs, half, HEAD_W), f32),
        compiler_params=_cparams(("parallel", "arbitrary")),
        name="df_sample",
    )(page_table, lamv, q, kn, vn, zr, sub_norm.reshape(1, HEAD_W), erep, *([cache_kt] * PPS), *([cache_v] * PPS))
    return out.reshape(bs, ls, ATT_INNER)


def _even_weights(w_in, w_out):
    o = 0
    segs = []
    for size in (M_INNER, M_CONV_DIM, M_HEADS, ATT_INNER, ATT_INNER, ATT_INNER, ATT_INNER):
        segs.append(w_in[:, o:o + size].astype(bf16))
        o += size
    segs[2] = jnp.concatenate([segs[2], jnp.zeros((D_MODEL, LANES - M_HEADS), bf16)], axis=1)
    return segs, w_out.astype(bf16)


def _even_layer(x, b, l, h0, conv0, past, norm_w, wsegs, w_out, conv_w, conv_b, dt_bias, a_log, d_skip, m_norm):
    w_zm, w_xbc, w_dt, w_zsb, w_q, w_k, w_v = wsegs
    xn = _rmsnorm(x, norm_w)
    z_m = _matmul([xn], w_zm)
    xbc = _matmul([xn], w_xbc)
    dt = _matmul([xn], w_dt)
    z_sb = _matmul([xn], w_zsb)
    xbc3 = xbc.reshape(b, l, M_CONV_DIM)
    y_m, h_new = _ssd(xbc3, z_m.reshape(b, l, M_INNER), dt.reshape(b, l, LANES), conv0, h0,
                      conv_w, conv_b, dt_bias, a_log, d_skip, m_norm)
    conv_new = jnp.concatenate([conv0, xbc3], axis=1)[:, -(M_CONV - 1):]
    if past is None:
        q_b = _matmul([xn], w_q, out_kinds=(bf16,))
        k, k_b = _matmul([xn], w_k, out_kinds=(f32, bf16))
        v, v_b = _matmul([xn], w_v, out_kinds=(f32, bf16))
        y_sb = _sb_prompt(q_b, k_b, v_b, z_sb, b, l)
    else:
        q = _matmul([xn], w_q)
        k = _matmul([xn], w_k)
        v = _matmul([xn], w_v)
        r3 = lambda t: t.reshape(b, l, ATT_INNER)
        y_sb = _sb_sample(r3(q), r3(k), r3(v), r3(z_sb), *past).reshape(b * l, ATT_INNER).astype(bf16)
    x_new = _matmul([y_m.reshape(b * l, M_INNER), y_sb], w_out, res=x, tn=512)
    return x_new, h_new, conv_new, k, v


def _odd_layer(x, b, l, past, lam_init, norm_w, wsegs, w_out, lamv, sub_norm, norm_f):
    w_z, w_q, w_k, w_v = wsegs
    xn = _rmsnorm(x, norm_w)
    z = _matmul([xn], w_z)
    if past is None:
        q_b = _matmul([xn], w_q, out_kinds=(bf16,))
        k, k_b = _matmul([xn], w_k, out_kinds=(f32, bf16))
        v, v_b = _matmul([xn], w_v, out_kinds=(f32, bf16))
        y = _df_prompt(lamv, q_b, k_b, v_b, z, sub_norm, b, l, lam_init)
    else:
        q = _matmul([xn], w_q)
        k = _matmul([xn], w_k)
        v = _matmul([xn], w_v)
        r3 = lambda t: t.reshape(b, l, ATT_INNER)
        y = _df_sample(lamv, r3(q), r3(k), r3(v), r3(z), sub_norm, *past, lam_init)
        y = y.reshape(b * l, ATT_INNER).astype(bf16)
    out = _matmul([y], w_out, res=x, norm_w=norm_f, tm=512)
    return out, k, v


def kernel(x_prompt, x_sample, state_ssm, state_conv, cache_sb_k, cache_sb_v, cache_df_k, cache_df_v, page_table,
           norm_even, w_in_even, conv_w, conv_b, dt_bias, a_log, d_skip, m_norm, w_out_even,
           norm_odd, w_in_odd, lambda_q1, lambda_k1, lambda_q2, lambda_k2, df_norm, w_out_odd, norm_f):
    bp, lp, d = x_prompt.shape
    bs, ls, _ = x_sample.shape
    n_pool = cache_sb_k.shape[1]
    xp = x_prompt.reshape(bp * lp, d)
    xs = x_sample.reshape(bs * ls, d)

    wsegs, w_out = _even_weights(w_in_even[0], w_out_even[0])
    ew = (norm_even[0], wsegs, w_out, conv_w[0], conv_b[0], dt_bias[0], a_log[0], d_skip[0], m_norm[0])
    h0 = jnp.zeros((bp,) + state_ssm.shape[2:], f32)
    c0 = jnp.zeros((bp,) + state_conv.shape[2:], f32)
    xp, ssm_p, conv_p, sbk_p, sbv_p = _even_layer(xp, bp, lp, h0, c0, None, *ew)
    past = (cache_sb_k[0], cache_sb_v[0], page_table)
    xs, ssm_s, conv_s, sbk_s, sbv_s = _even_layer(xs, bs, ls, state_ssm[0], state_conv[0], past, *ew)

    lam_init = 0.8 - 0.6 * math.exp(-0.3 * 1)
    w_in = w_in_odd[0]
    osegs = [w_in[:, ATT_INNER * s:ATT_INNER * (s + 1)].astype(bf16) for s in range(4)]
    lamv = jnp.zeros((8, LANES), f32)
    lamv = lamv.at[0, :DF_HALF].set(lambda_q1[0]).at[1, :DF_HALF].set(lambda_k1[0])
    lamv = lamv.at[2, :DF_HALF].set(lambda_q2[0]).at[3, :DF_HALF].set(lambda_k2[0])
    ow = (norm_odd[0], osegs, w_out_odd[0].astype(bf16), lamv, df_norm[0], norm_f)
    y_prompt, dfk_p, dfv_p = _odd_layer(xp, bp, lp, None, lam_init, *ow)
    cache_kt = jnp.transpose(cache_df_k[0], (0, 2, 3, 4, 1)).reshape(n_pool, ATT_INNER, PAGE)
    past = (cache_kt, cache_df_v[0], page_table)
    y_sample, dfk_s, dfv_s = _odd_layer(xs, bs, ls, past, lam_init, *ow)

    hd = (N_HEADS, HEAD_W)
    hd2 = (N_HEADS, 2, DF_HALF)
    return (y_prompt.reshape(bp, lp, d), y_sample.reshape(bs, ls, d),
            ssm_p[None], conv_p[None],
            sbk_p.reshape((1, bp, lp) + hd), sbv_p.reshape((1, bp, lp) + hd),
            dfk_p.reshape((1, bp, lp) + hd2), dfv_p.reshape((1, bp, lp) + hd),
            ssm_s[None], conv_s[None],
            sbk_s.reshape((1, bs, ls) + hd), sbv_s.reshape((1, bs, ls) + hd),
            dfk_s.reshape((1, bs, ls) + hd2), dfv_s.reshape((1, bs, ls) + hd))
```
